```python
import math
import jax, jax.numpy as jnp
from jax import lax
import numpy as np

D_MODEL = 1024
BATCH = 16
SEQ = 256
DEPTH = 2
DEC_BATCH = 8
DEC_SEQ = 4096
PAST_LEN = 512

GRID_W = 64
N_MIXERS = 2
N_HYENA_LAYERS = (DEPTH + 1) // 2
N_ATTN_LAYERS = DEPTH // 2
FILTER_WIDTH = 64
N_BANDS = 8
FILTER_EMB = 1 + 2 * N_BANDS
DECAY_MIN = math.log(1e-2) / 1.5
DECAY_MAX = math.log(1e-2) / 0.3
HEAD_DIM = 64
N_HEADS = D_MODEL // (2 * HEAD_DIM)
V_DIM = 2 * HEAD_DIM
ROPE_THETA = 10000.0
ROPE_AXIS_PAIRS = HEAD_DIM // 4
Q_BLOCK = 128
N_EXPERTS = 32
TOP_K = 4
D_FF = D_MODEL
SWIGLU_LIMIT = 7.0
SWIGLU_ALPHA = 1.702
MOE_BLOCK = 256
EPS = 1e-6

kernel_name = 'hybrid_hyena_diffattn_moe_step'


def rmsnorm(x, g):
    xf = x.astype(jnp.float32)
    y = xf * lax.rsqrt(jnp.mean(xf * xf, axis=-1, keepdims=True) + EPS)
    return (y * g.astype(jnp.float32)).astype(x.dtype)


def adaln_params(cond, w_mod, b_mod):
    m = jax.nn.silu(cond) @ w_mod + b_mod
    return jnp.split(m[:, None, :], 6, axis=-1)


def modulate(x, g, shift, scale):
    return rmsnorm(x, g) * (1 + scale) + shift


def short_conv(u, w, b):
    up = jnp.pad(u, ((0, 0), (1, 1), (0, 0)))
    return up[:, :-2] * w[0] + up[:, 1:-1] * w[1] + up[:, 2:] * w[2] + b


def hyena_filters(L, f_w1, f_b1, f_w2, f_b2, f_w3, f_freq):
    f = lambda a: a.astype(jnp.float32)
    pos = jnp.arange(L, dtype=jnp.float32)
    t = pos / L
    bands = jnp.linspace(1e-4, N_BANDS - 1, N_BANDS, dtype=jnp.float32)
    ang = 2.0 * jnp.pi * t[:, None] * bands
    feats = jnp.concatenate([t[:, None], jnp.cos(ang), jnp.sin(ang)], axis=-1)
    a = jnp.sin(f(f_freq[0]) * (feats @ f(f_w1) + f(f_b1)))
    a = jnp.sin(f(f_freq[1]) * (a @ f(f_w2) + f(f_b2)))
    h = (a @ f(f_w3)).reshape(L, 2, D_MODEL)
    deltas = jnp.abs(jnp.linspace(DECAY_MIN, DECAY_MAX, D_MODEL, dtype=jnp.float32))
    h = h * jnp.exp(-t[:, None, None] * deltas)
    return h[:, 0], h[:, 1]


def long_conv(z, h_fwd, h_bwd, d_skip):
    L = z.shape[1]
    k = jnp.concatenate([h_fwd, jnp.zeros((1, D_MODEL), jnp.float32), h_bwd[:0:-1]], axis=0)
    zf = jnp.fft.rfft(z.astype(jnp.float32), n=2 * L, axis=1)
    kf = jnp.fft.rfft(k, axis=0)
    y = jnp.fft.irfft(zf * kf[None], n=2 * L, axis=1)[:, :L]
    return (y + z.astype(jnp.float32) * d_skip.astype(jnp.float32)).astype(z.dtype)


def hyena_mixer(h, w_in, b_in, w_short, b_short, f_w1, f_b1, f_w2, f_b2, f_w3, f_freq, d_skip, w_out, b_out):
    L = h.shape[1]
    u = short_conv(h @ w_in + b_in, w_short, b_short)
    x0, x1, v = jnp.split(u, 3, axis=-1)
    h_fwd, h_bwd = hyena_filters(L, f_w1, f_b1, f_w2, f_b2, f_w3, f_freq)
    z = long_conv(v * x1, h_fwd, h_bwd, d_skip)
    return (z * x0) @ w_out + b_out


def axial_rope(n_tokens):
    n_rows = n_tokens // GRID_W
    row = jnp.repeat(jnp.arange(n_rows, dtype=jnp.float32), GRID_W)
    col = jnp.tile(jnp.arange(GRID_W, dtype=jnp.float32), n_rows)
    inv = ROPE_THETA ** (-jnp.arange(ROPE_AXIS_PAIRS, dtype=jnp.float32) / ROPE_AXIS_PAIRS)
    ang = jnp.concatenate([row[:, None] * inv, col[:, None] * inv], axis=-1)
    return jnp.cos(ang), jnp.sin(ang)


def apply_rope(x, cos, sin):
    cos = cos[None, :, None, None, :]
    sin = sin[None, :, None, None, :]
    half = HEAD_DIM // 2
    xf = x.astype(jnp.float32)
    x1, x2 = xf[..., :half], xf[..., half:]
    return jnp.concatenate([x1 * cos - x2 * sin, x2 * cos + x1 * sin], axis=-1).astype(x.dtype)


def diff_attn_project(h, w_qkv, g_q, g_k):
    B, L, _ = h.shape
    qkv = h @ w_qkv
    qk_w = N_HEADS * 2 * HEAD_DIM
    q = qkv[..., :qk_w].reshape(B, L, N_HEADS, 2, HEAD_DIM)
    k = qkv[..., qk_w:2 * qk_w].reshape(B, L, N_HEADS, 2, HEAD_DIM)
    v = qkv[..., 2 * qk_w:].reshape(B, L, N_HEADS, V_DIM)
    return rmsnorm(q, g_q), rmsnorm(k, g_k), v


def diff_lambda(lq1, lk1, lq2, lk2, lam_init):
    f = lambda a: a.astype(jnp.float32)
    return jnp.exp(jnp.sum(f(lq1) * f(lk1))) - jnp.exp(jnp.sum(f(lq2) * f(lk2))) + lam_init


def diff_attention(q, k, v, lam):
    B, Lq = q.shape[0], q.shape[1]
    n_blk = Lq // Q_BLOCK
    qb = jnp.moveaxis(q.reshape(B, n_blk, Q_BLOCK, N_HEADS, 2, HEAD_DIM), 1, 0)
    scale = HEAD_DIM ** -0.5

    def block(q_blk):
        s = jnp.einsum('bqhmd,bkhmd->bmhqk', q_blk, k).astype(jnp.float32) * scale
        p = jax.nn.softmax(s, axis=-1)
        a = p[:, 0] - lam * p[:, 1]
        return jnp.einsum('bhqk,bkhe->bqhe', a.astype(v.dtype), v)

    o = lax.map(block, qb)
    return jnp.moveaxis(o, 0, 1).reshape(B, Lq, N_HEADS, V_DIM)


def diff_attn_merge(o, g_sub, lam_init, w_o):
    B, L = o.shape[0], o.shape[1]
    o = rmsnorm(o, g_sub) * (1.0 - lam_init)
    return o.reshape(B, L, N_HEADS * V_DIM) @ w_o


def moe_ffn(x, w_router, b_router, w_in, b_in, w_out, b_out):
    B, L, D = x.shape
    xt = x.reshape(B * L, D)
    N = B * L
    logits = xt.astype(jnp.float32) @ w_router.astype(jnp.float32) + b_router.astype(jnp.float32)
    top_vals, top_idx = lax.top_k(logits, TOP_K)
    gates = jax.nn.softmax(top_vals, axis=-1)
    A = N * TOP_K
    flat_e = top_idx.reshape(A)
    flat_tok = jnp.arange(A) // TOP_K
    order = jnp.argsort(flat_e)
    sorted_e = flat_e[order]
    sorted_tok = flat_tok[order]
    sorted_g = gates.reshape(A)[order]
    counts = jnp.bincount(flat_e, length=N_EXPERTS)
    starts = jnp.cumsum(counts) - counts
    padded = ((counts + MOE_BLOCK - 1) // MOE_BLOCK) * MOE_BLOCK
    pends = jnp.cumsum(padded)
    pstarts = pends - padded
    dest = pstarts[sorted_e] + (jnp.arange(A) - starts[sorted_e])
    P = -(-A // MOE_BLOCK) * MOE_BLOCK + N_EXPERTS * MOE_BLOCK
    n_blk = P // MOE_BLOCK
    block_e = jnp.minimum(jnp.searchsorted(pends, jnp.arange(n_blk) * MOE_BLOCK, side='right'), N_EXPERTS - 1)
    xb = jnp.zeros((P, D), x.dtype).at[dest].set(xt[sorted_tok])

    def expert_block(args):
        xblk, e = args
        gu = xblk @ w_in[e] + b_in[e]
        gate = jnp.minimum(gu[:, :D_FF], SWIGLU_LIMIT)
        up = jnp.clip(gu[:, D_FF:], -SWIGLU_LIMIT, SWIGLU_LIMIT)
        glu = gate * jax.nn.sigmoid(SWIGLU_ALPHA * gate)
        return ((up + 1) * glu) @ w_out[e] + b_out[e]

    yb = lax.map(expert_block, (xb.reshape(n_blk, MOE_BLOCK, D), block_e)).reshape(P, D)
    contrib = yb[dest] * sorted_g[:, None].astype(x.dtype)
    out = jax.ops.segment_sum(contrib, sorted_tok, num_segments=N)
    return out.reshape(B, L, D)


def setup_inputs(seed: int = 0) -> dict:
    key = jax.random.key(seed)
    counter = [0]

    def nrm(shape, scale=1.0):
        counter[0] += 1
        return jax.random.normal(jax.random.fold_in(key, counter[0]), shape, jnp.float32) * scale

    D, E, F = D_MODEL, N_EXPERTS, D_FF
    NH, NA = N_HYENA_LAYERS, N_ATTN_LAYERS
    QKV = 2 * N_HEADS * 2 * HEAD_DIM + N_HEADS * V_DIM
    return {
        'x_prompt': nrm((BATCH, SEQ, D)),
        'x_sample': nrm((DEC_BATCH, DEC_SEQ, D)),
        'cache_k': nrm((DEC_BATCH, NA, PAST_LEN, N_HEADS, 2 * HEAD_DIM)),
        'cache_v': nrm((DEC_BATCH, NA, PAST_LEN, N_HEADS, V_DIM)),
        'c': nrm((DEC_BATCH, D)),
        'c_ctx': nrm((D,)),
        'w_mod': nrm((DEPTH, D, 6 * D), D ** -0.5),
        'b_mod': nrm((DEPTH, 6 * D), 0.01),
        'g_mix': 1.0 + nrm((DEPTH, D), 0.01),
        'g_ffn': 1.0 + nrm((DEPTH, D), 0.01),
        'hy_w_in': nrm((NH, D, 3 * D), D ** -0.5),
        'hy_b_in': nrm((NH, 3 * D), 0.01),
        'hy_w_short': nrm((NH, 3, 3 * D), 3 ** -0.5),
        'hy_b_short': nrm((NH, 3 * D), 0.01),
        'hy_f_w1': nrm((NH, FILTER_EMB, FILTER_WIDTH), FILTER_EMB ** -0.5),
        'hy_f_b1': nrm((NH, FILTER_WIDTH), 0.01),
        'hy_f_w2': nrm((NH, FILTER_WIDTH, FILTER_WIDTH), FILTER_WIDTH ** -0.5),
        'hy_f_b2': nrm((NH, FILTER_WIDTH), 0.01),
        'hy_f_w3': nrm((NH, FILTER_WIDTH, 2 * D), 0.05 * FILTER_WIDTH ** -0.5),
        'hy_f_freq': 1.0 + nrm((NH, 2, FILTER_WIDTH), 0.1),
        'hy_d_skip': nrm((NH, D), 0.5),
        'hy_w_out': nrm((NH, D, D), D ** -0.5),
        'hy_b_out': nrm((NH, D), 0.01),
        'at_w_qkv': nrm((NA, D, QKV), D ** -0.5),
        'at_g_q': 1.0 + nrm((NA, HEAD_DIM), 0.01),
        'at_g_k': 1.0 + nrm((NA, HEAD_DIM), 0.01),
        'at_lq1': nrm((NA, HEAD_DIM), 0.1),
        'at_lk1': nrm((NA, HEAD_DIM), 0.1),
        'at_lq2': nrm((NA, HEAD_DIM), 0.1),
        'at_lk2': nrm((NA, HEAD_DIM), 0.1),
        'at_g_sub': 1.0 + nrm((NA, V_DIM), 0.01),
        'at_w_o': nrm((NA, N_HEADS * V_DIM, D), (N_HEADS * V_DIM) ** -0.5),
        'moe_w_router': nrm((DEPTH, D, E), D ** -0.5),
        'moe_b_router': nrm((DEPTH, E), 0.01),
        'moe_w_in': nrm((DEPTH, E, D, 2 * F), D ** -0.5),
        'moe_b_in': nrm((DEPTH, E, 2 * F), 0.01),
        'moe_w_out': nrm((DEPTH, E, F, D), F ** -0.5),
        'moe_b_out': nrm((DEPTH, E, D), 0.01),
    }


def reference(x_prompt, x_sample, cache_k, cache_v, c, c_ctx, w_mod, b_mod, g_mix, g_ffn,
              hy_w_in, hy_b_in, hy_w_short, hy_b_short, hy_f_w1, hy_f_b1, hy_f_w2, hy_f_b2,
              hy_f_w3, hy_f_freq, hy_d_skip, hy_w_out, hy_b_out,
              at_w_qkv, at_g_q, at_g_k, at_lq1, at_lk1, at_lq2, at_lk2, at_g_sub, at_w_o,
              moe_w_router, moe_b_router, moe_w_in, moe_b_in, moe_w_out, moe_b_out):
    cos_l, sin_l = axial_rope(x_sample.shape[1])
    ctx, lat = x_prompt, x_sample
    new_k, new_v = [], []
    for i in range(DEPTH):
        sh1c, sc1c, g1c, sh2c, sc2c, g2c = adaln_params(c_ctx[None, :], w_mod[i], b_mod[i])
        sh1l, sc1l, g1l, sh2l, sc2l, g2l = adaln_params(c, w_mod[i], b_mod[i])
        hc = modulate(ctx, g_mix[i], sh1c, sc1c)
        hl = modulate(lat, g_mix[i], sh1l, sc1l)
        j = i // N_MIXERS
        if i % N_MIXERS == 0:
            hy = (hy_w_in[j], hy_b_in[j], hy_w_short[j], hy_b_short[j], hy_f_w1[j], hy_f_b1[j],
                  hy_f_w2[j], hy_f_b2[j], hy_f_w3[j], hy_f_freq[j], hy_d_skip[j], hy_w_out[j], hy_b_out[j])
            mc = hyena_mixer(hc, *hy)
            ml = hyena_mixer(hl, *hy)
        else:
            lam_init = 0.8 - 0.6 * math.exp(-0.3 * i)
            lam = diff_lambda(at_lq1[j], at_lk1[j], at_lq2[j], at_lk2[j], lam_init)
            qc, kc, vc = diff_attn_project(hc, at_w_qkv[j], at_g_q[j], at_g_k[j])
            ql, kl, vl = diff_attn_project(hl, at_w_qkv[j], at_g_q[j], at_g_k[j])
            ql = apply_rope(ql, cos_l, sin_l)
            kl = apply_rope(kl, cos_l, sin_l)
            new_k.append(kc.reshape(kc.shape[0], kc.shape[1], N_HEADS, 2 * HEAD_DIM))
            new_v.append(vc)
            ck = cache_k[:, j].reshape(cache_k.shape[0], cache_k.shape[2], N_HEADS, 2, HEAD_DIM)
            cv = cache_v[:, j]
            mc = diff_attn_merge(diff_attention(qc, kc, vc, lam), at_g_sub[j], lam_init, at_w_o[j])
            ol = diff_attention(ql, jnp.concatenate([kl, ck], axis=1), jnp.concatenate([vl, cv], axis=1), lam)
            ml = diff_attn_merge(ol, at_g_sub[j], lam_init, at_w_o[j])
        ctx = ctx + g1c * mc
        lat = lat + g1l * ml
        moe_p = (moe_w_router[i], moe_b_router[i], moe_w_in[i], moe_b_in[i], moe_w_out[i], moe_b_out[i])
        ctx = ctx + g2c * moe_ffn(modulate(ctx, g_ffn[i], sh2c, sc2c), *moe_p)
        lat = lat + g2l * moe_ffn(modulate(lat, g_ffn[i], sh2l, sc2l), *moe_p)
    y_prompt = ctx
    y_sample = lat
    new_cache_k = jnp.stack(new_k, axis=1)
    new_cache_v = jnp.stack(new_v, axis=1)
    return (y_prompt, y_sample, new_cache_k, new_cache_v)
```

```python
import functools
import math

import numpy as np
import jax
import jax.numpy as jnp
from jax import lax
from jax.experimental import pallas as pl
from jax.experimental.pallas import tpu as pltpu

F32 = jnp.float32
BF16 = jnp.bfloat16
HIGHEST = lax.Precision.HIGHEST

EPS = 1e-6
LANES = 128
TM = 256
HALO = 8
N_COND = 16
GRID_W = 64
HEAD_DIM = 64
V_DIM = 2 * HEAD_DIM
ROPE_THETA = 10000.0
N_BANDS = 8
DECAY_MIN = math.log(1e-2) / 1.5
DECAY_MAX = math.log(1e-2) / 0.3
TOP_K = 4
SWIGLU_LIMIT = 7.0
SWIGLU_ALPHA = 1.702
MOE_BLOCK = 256
CONV_CHUNK = 512
CONV_DT = 128
VMEM_LIMIT = 56 * 1024 * 1024


def _cparams(sem):
    return pltpu.CompilerParams(dimension_semantics=sem, vmem_limit_bytes=VMEM_LIMIT)


def _dot(a, b):
    return jnp.dot(a, b, preferred_element_type=F32)


def _dot_hi(a, b):
    return jnp.dot(a, b, preferred_element_type=F32, precision=HIGHEST)


def _modulate(x, g, shift, scale):
    ms = jnp.mean(x * x, axis=-1, keepdims=True)
    y = x * lax.rsqrt(ms + EPS)
    return (y * g) * (1.0 + scale) + shift


def _sigmoid(x):
    return 1.0 / (1.0 + jnp.exp(-x))


def _mod_kernel(c_ref, w_ref, b_ref, o_ref):
    c = c_ref[...]
    s = c * _sigmoid(c)
    o_ref[...] = _dot_hi(s, w_ref[...]) + b_ref[...]


def adaln_params(cond, w_mod, b_mod):
    depth, d, d6 = w_mod.shape
    tn = d
    out = pl.pallas_call(
        _mod_kernel,
        grid=(depth, d6 // tn),
        in_specs=[
            pl.BlockSpec((N_COND, d), lambda l, j: (0, 0)),
            pl.BlockSpec((None, d, tn), lambda l, j: (l, 0, j)),
            pl.BlockSpec((None, 1, tn), lambda l, j: (l, 0, j)),
        ],
        out_specs=pl.BlockSpec((None, N_COND, tn), lambda l, j: (l, 0, j)),
        out_shape=jax.ShapeDtypeStruct((depth, N_COND, d6), F32),
        compiler_params=_cparams(("parallel", "parallel")),
        name="adaln_params",
    )(cond, w_mod, b_mod.reshape(depth, 1, d6))
    return out.reshape(depth, N_COND, 6, 1, d)


def _mod_spec(layer, j, d):
    return pl.BlockSpec((None, None, None, 1, d), lambda i, cid, *_: (layer, cid[i], j, 0, 0))


def _row_spec(d):
    return pl.BlockSpec((1, d), lambda i, *_: (0, 0))


def _hyena_in_kernel(cid_ref, hp_ref, hn_ref, xp_ref, xm_ref, xn_ref, sh_ref, sc_ref, g_ref,
                     w_ref, b_ref, ws_ref, bs_ref, x0_ref, z_ref, *, ct):
    i = pl.program_id(0)
    d = xm_ref.shape[1]
    rows = TM + 2 * HALO
    xa = jnp.concatenate([xp_ref[...], xm_ref[...], xn_ref[...]], axis=0)
    h = _modulate(xa, g_ref[...], sh_ref[...], sc_ref[...]).astype(BF16)
    r = lax.broadcasted_iota(jnp.int32, (rows, 1), 0)
    valid = ((r >= HALO) & (r < HALO + TM)) | ((r == HALO - 1) & (hp_ref[i] > 0)) | (
        (r == HALO + TM) & (hn_ref[i] > 0))
    for c in range(d // ct):
        conv = []
        for grp in range(3):
            lo = grp * d + c * ct
            u = _dot(h, w_ref[:, lo:lo + ct]) + b_ref[:, lo:lo + ct]
            u = jnp.where(valid, u, 0.0)
            w3 = ws_ref[:, lo:lo + ct]
            y = (pltpu.roll(u, 1, 0) * w3[0:1] + u * w3[1:2] + pltpu.roll(u, rows - 1, 0) * w3[2:3]
                 + bs_ref[:, lo:lo + ct])
            conv.append(y[HALO:HALO + TM])
        x0_ref[:, c * ct:(c + 1) * ct] = conv[0].astype(BF16)
        z_ref[:, c * ct:(c + 1) * ct] = (conv[2] * conv[1]).astype(BF16)


def hyena_in(x, mod, layer, meta, g, w_in, b_in, w_short, b_short):
    t, d = x.shape
    nblk = t // TM
    per = TM // HALO
    nh = t // HALO
    grid_spec = pltpu.PrefetchScalarGridSpec(
        num_scalar_prefetch=3,
        grid=(nblk,),
        in_specs=[
            pl.BlockSpec((HALO, d), lambda i, *_: (jnp.maximum(i * per - 1, 0), 0)),
            pl.BlockSpec((TM, d), lambda i, *_: (i, 0)),
            pl.BlockSpec((HALO, d), lambda i, *_: (jnp.minimum(i * per + per, nh - 1), 0)),
            _mod_spec(layer, 0, d),
            _mod_spec(layer, 1, d),
            _row_spec(d),
            pl.BlockSpec((d, 3 * d), lambda i, *_: (0, 0)),
            _row_spec(3 * d),
            pl.BlockSpec((3, 3 * d), lambda i, *_: (0, 0)),
            _row_spec(3 * d),
        ],
        out_specs=[pl.BlockSpec((TM, d), lambda i, *_: (i, 0)),
                   pl.BlockSpec((TM, d), lambda i, *_: (i, 0))],
    )
    return pl.pallas_call(
        functools.partial(_hyena_in_kernel, ct=256),
        grid_spec=grid_spec,
        out_shape=[jax.ShapeDtypeStruct((t, d), BF16), jax.ShapeDtypeStruct((t, d), BF16)],
        compiler_params=_cparams(("parallel",)),
        name="hyena_in",
    )(meta["cid"], meta["has_prev"], meta["has_next"], x, x, x, mod, mod, g.reshape(1, d),
      w_in.astype(BF16), b_in.reshape(1, 3 * d), w_short, b_short.reshape(1, 3 * d))


def _taps_kernel(bv_ref, dl_ref, w1_ref, b1_ref, w2_ref, b2_ref, w3_ref, fr_ref, o_ref, *, seq, rb):
    i = pl.program_id(0)
    d = o_ref.shape[1]
    j = i * rb + lax.broadcasted_iota(jnp.int32, (rb, 1), 0)
    tau = j - seq
    t = jnp.abs(tau).astype(F32) / seq
    lane = lax.broadcasted_iota(jnp.int32, (rb, LANES), 1)
    ang = (2.0 * jnp.pi * t) * bv_ref[...]
    feats = jnp.where(lane == 0, t,
                      jnp.where(lane <= N_BANDS, jnp.cos(ang),
                                jnp.where(lane <= 2 * N_BANDS, jnp.sin(ang), 0.0)))
    a = jnp.sin(fr_ref[0:1] * (_dot_hi(feats, w1_ref[...]) + b1_ref[...]))
    a = jnp.sin(fr_ref[1:2] * (_dot_hi(a, w2_ref[...]) + b2_ref[...]))
    hh = _dot_hi(a, w3_ref[...])
    decay = jnp.exp(-t * dl_ref[...])
    taps = jnp.where(tau >= 0, hh[:, :d], hh[:, d:]) * decay
    o_ref[...] = jnp.where(tau == -seq, 0.0, taps)


def hyena_taps(seq, f_w1, f_b1, f_w2, f_b2, f_w3, f_freq):
    emb, width = f_w1.shape
    d = f_w3.shape[1] // 2
    rb = 256
    bands = jnp.linspace(1e-4, N_BANDS - 1, N_BANDS, dtype=F32)
    bv = jnp.zeros((1, LANES), F32).at[0, 1:1 + N_BANDS].set(bands).at[0, 1 + N_BANDS:1 + 2 * N_BANDS].set(bands)
    deltas = jnp.abs(jnp.linspace(DECAY_MIN, DECAY_MAX, d, dtype=F32)).reshape(1, d)
    w1p = jnp.zeros((LANES, width), F32).at[:emb].set(f_w1)
    full = lambda shape: pl.BlockSpec(shape, lambda i: (0,) * len(shape))
    return pl.pallas_call(
        functools.partial(_taps_kernel, seq=seq, rb=rb),
        grid=(2 * seq // rb,),
        in_specs=[full((1, LANES)), full((1, d)), full((LANES, width)), full((1, width)),
                  full((width, width)), full((1, width)), full((width, 2 * d)), full((2, width))],
        out_specs=pl.BlockSpec((rb, d), lambda i: (i, 0)),
        out_shape=jax.ShapeDtypeStruct((2 * seq, d), F32),
        compiler_params=_cparams(("parallel",)),
        name="hyena_taps",
    )(bv, deltas, w1p, f_b1.reshape(1, width), f_w2, f_b2.reshape(1, width), f_w3, f_freq)


def _dft_constants(b):
    n = 2 * b
    f = np.arange(b, dtype=np.float64)[:, None]
    s = np.arange(b, dtype=np.float64)[None, :]
    th = 2.0 * np.pi * f * s / n
    fre = np.cos(th)
    fim = -np.sin(th)
    fim[0, :] = np.cos(np.pi * s[0])
    tt = np.arange(b, dtype=np.float64)[:, None]
    ff = np.arange(b, dtype=np.float64)[None, :]
    th = 2.0 * np.pi * tt * ff / n
    gre = (2.0 / n) * np.cos(th)
    gre[:, 0] = 1.0 / n
    gim = -(2.0 / n) * np.sin(th)
    gim[:, 0] = np.cos(np.pi * tt[:, 0]) / n
    rr = (np.arange(n, dtype=np.float64) - b)[None, :]
    th = 2.0 * np.pi * f * rr / n
    cm = np.cos(th)
    sm = -np.sin(th)
    dm = cm.copy()
    dm[0, :] = np.cos(np.pi * rr[0])
    for m in (cm, sm, dm):
        m[:, 0] = 0.0
    fwd = np.stack([fre, fim, gre, gim]).astype(np.float32)
    seg = np.stack([cm, sm, dm]).astype(np.float32)
    return fwd, seg


def _spectra_kernel(m_ref, t0_ref, t1_ref, a_ref, b_ref, d_ref):
    seg = jnp.concatenate([t0_ref[...], t1_ref[...]], axis=0).astype(BF16)
    a_ref[...] = _dot(m_ref[0], seg)
    b_ref[...] = _dot(m_ref[1], seg)
    d_ref[...] = _dot(m_ref[2], seg)


def hyena_spectra(taps, b, seg_mats):
    rows, d = taps.shape
    nseg = rows // b - 1
    dt = 256
    out = jax.ShapeDtypeStruct((nseg, b, d), F32)
    ospec = pl.BlockSpec((None, b, dt), lambda s, c: (s, 0, c))
    return pl.pallas_call(
        _spectra_kernel,
        grid=(nseg, d // dt),
        in_specs=[pl.BlockSpec((3, b, 2 * b), lambda s, c: (0, 0, 0)),
                  pl.BlockSpec((b, dt), lambda s, c: (s, c)),
                  pl.BlockSpec((b, dt), lambda s, c: (s + 1, c))],
        out_specs=[ospec, ospec, ospec],
        out_shape=[out, out, out],
        compiler_params=_cparams(("parallel", "parallel")),
        name="hyena_spectra",
    )(seg_mats, taps, taps)


def _conv_kernel(z_ref, a_ref, bz_ref, dd_ref, f_ref, o_ref, zre_ref, zim_ref, yre_ref, yim_ref, *, n, b, ft):
    dt = z_ref.shape[1]
    for i in range(n):
        zi = z_ref[i * b:(i + 1) * b, :]
        zre_ref[i] = _dot(f_ref[0], zi)
        zim_ref[i] = _dot(f_ref[1], zi)

    def m_body(m, carry):
        def ft_body(q, carry2):
            r0 = pl.multiple_of(q * ft, ft)
            accre = jnp.zeros((ft, dt), F32)
            accim = jnp.zeros((ft, dt), F32)
            for i in range(n):
                dseg = m - i + (n - 1)
                ka = a_ref[dseg, pl.ds(r0, ft), :]
                kb = bz_ref[dseg, pl.ds(r0, ft), :]
                kd = dd_ref[dseg, pl.ds(r0, ft), :]
                zr = zre_ref[i, pl.ds(r0, ft), :]
                zi = zim_ref[i, pl.ds(r0, ft), :]
                accre = accre + (zr * ka - zi * kb)
                accim = accim + (zr * kb + zi * kd)
            yre_ref[pl.ds(r0, ft), :] = accre.astype(BF16)
            yim_ref[pl.ds(r0, ft), :] = accim.astype(BF16)
            return carry2

        lax.fori_loop(0, b // ft, ft_body, 0)
        y = _dot(f_ref[2], yre_ref[...]) + _dot(f_ref[3], yim_ref[...])
        o_ref[pl.ds(pl.multiple_of(m * b, b), b), :] = y.astype(BF16)
        return carry

    lax.fori_loop(0, n, m_body, 0)


def hyena_long_conv(z, spectra, fwd_mats, seq, nbatch, row0):
    t, d = z.shape
    a, bz, dd = spectra
    nseg, b, _ = a.shape
    n = (nseg + 1) // 2
    assert n * b == seq and row0 % seq == 0
    blk0 = row0 // seq
    dt = CONV_DT
    kspec = pl.BlockSpec((nseg, b, dt), lambda c, s: (0, 0, c))
    return pl.pallas_call(
        functools.partial(_conv_kernel, n=n, b=b, ft=64),
        grid=(d // dt, nbatch),
        in_specs=[pl.BlockSpec((seq, dt), lambda c, s: (blk0 + s, c)), kspec, kspec, kspec,
                  pl.BlockSpec((4, b, b), lambda c, s: (0, 0, 0))],
        out_specs=pl.BlockSpec((seq, dt), lambda c, s: (s, c)),
        out_shape=jax.ShapeDtypeStruct((nbatch * seq, d), BF16),
        scratch_shapes=[pltpu.VMEM((n, b, dt), F32), pltpu.VMEM((n, b, dt), F32),
                        pltpu.VMEM((b, dt), BF16), pltpu.VMEM((b, dt), BF16)],
        compiler_params=_cparams(("parallel", "arbitrary")),
        name="hyena_long_conv",
    )(z, a, bz, dd, fwd_mats)


def _top4_route(logits, n_experts):
    lane = lax.broadcasted_iota(jnp.int32, logits.shape, 1)
    neg = jnp.float32(-jnp.inf)
    cur = jnp.where(lane < n_experts, logits, neg)
    vals, idxs = [], []
    for _ in range(TOP_K):
        m = jnp.max(cur, axis=-1, keepdims=True)
        idx = jnp.min(jnp.where(cur == m, lane, LANES), axis=-1, keepdims=True)
        vals.append(m)
        idxs.append(idx)
        cur = jnp.where(lane == idx, neg, cur)
    es = [jnp.exp(v - vals[0]) for v in vals]
    den = es[0] + es[1] + es[2] + es[3]
    out = jnp.zeros(logits.shape, F32)
    for k in range(TOP_K):
        out = jnp.where(lane == k, idxs[k].astype(F32), out)
        out = jnp.where(lane == TOP_K + k, es[k] / den, out)
    return out


def _tail_kernel(cid_ref, *refs, hyena, n_experts, ctx_blocks):
    is_ctx = pl.program_id(0) < ctx_blocks
    if hyena:
        (x_ref, yc_ref, yl_ref, z_ref, x0_ref, ds_ref, w_ref, b_ref, g1_ref, gf_ref, sh_ref, sc_ref,
         wrh_ref, wrl_ref, br_ref, xo_ref, hm_ref, rt_ref) = refs
        y = jnp.where(is_ctx, yc_ref[...], yl_ref[...])
        zz = (y.astype(F32) + z_ref[...].astype(F32) * ds_ref[...]) * x0_ref[...].astype(F32)
        m = _dot(zz.astype(BF16), w_ref[...]) + b_ref[...]
    else:
        (x_ref, oc_ref, ol_ref, w_ref, g1_ref, gf_ref, sh_ref, sc_ref,
         wrh_ref, wrl_ref, br_ref, xo_ref, hm_ref, rt_ref) = refs
        m = _dot(jnp.where(is_ctx, oc_ref[...], ol_ref[...]), w_ref[...])
    x1 = x_ref[...] + g1_ref[...] * m
    xo_ref[...] = x1
    hm = _modulate(x1, gf_ref[...], sh_ref[...], sc_ref[...])
    hm_ref[...] = hm
    hi = hm.astype(BF16)
    lo = (hm - hi.astype(F32)).astype(BF16)
    logits = _dot(hi, wrh_ref[...]) + _dot(hi, wrl_ref[...]) + _dot(lo, wrh_ref[...]) + br_ref[...]
    rt_ref[...] = _top4_route(logits, n_experts)


def mixer_tail(x, mix_ctx, mix_lat, extra, mod, layer, meta, w_out, b_out, g_ffn, w_router, b_router,
               d_skip=None):
    t, d = x.shape
    ncb = mix_ctx.shape[0] // TM
    cspec = pl.BlockSpec((TM, d), lambda i, *_: (jnp.minimum(i, ncb - 1), 0))
    lspec = pl.BlockSpec((TM, d), lambda i, *_: (jnp.maximum(i - ncb, 0), 0))
    n_experts = w_router.shape[1]
    hyena = d_skip is not None
    blk = pl.BlockSpec((TM, d), lambda i, *_: (i, 0))
    wr = jnp.zeros((d, LANES), F32).at[:, :n_experts].set(w_router)
    wrh = wr.astype(BF16)
    wrl = (wr - wrh.astype(F32)).astype(BF16)
    br = jnp.zeros((1, LANES), F32).at[0, :n_experts].set(b_router)
    wspec = pl.BlockSpec((d, d), lambda i, *_: (0, 0))
    rspec = pl.BlockSpec((d, LANES), lambda i, *_: (0, 0))
    if hyena:
        in_specs = [blk, cspec, lspec, blk, blk, _row_spec(d), wspec, _row_spec(d)]
        args = [x, mix_ctx, mix_lat, *extra, d_skip.reshape(1, d), w_out.astype(BF16), b_out.reshape(1, d)]
    else:
        in_specs = [blk, cspec, lspec, wspec]
        args = [x, mix_ctx, mix_lat, w_out.astype(BF16)]
    in_specs += [_mod_spec(layer, 2, d), _row_spec(d), _mod_spec(layer, 3, d), _mod_spec(layer, 4, d),
                 rspec, rspec, _row_spec(LANES)]
    args += [mod, g_ffn.reshape(1, d), mod, mod, wrh, wrl, br]
    grid_spec = pltpu.PrefetchScalarGridSpec(
        num_scalar_prefetch=1, grid=(t // TM,), in_specs=in_specs,
        out_specs=[blk, blk, pl.BlockSpec((TM, LANES), lambda i, *_: (i, 0))])
    return pl.pallas_call(
        functools.partial(_tail_kernel, hyena=hyena, n_experts=n_experts, ctx_blocks=ncb),
        grid_spec=grid_spec,
        out_shape=[jax.ShapeDtypeStruct((t, d), F32), jax.ShapeDtypeStruct((t, d), F32),
                   jax.ShapeDtypeStruct((t, LANES), F32)],
        compiler_params=_cparams(("parallel",)),
        name="mixer_tail",
    )(meta["cid"], *args)


def _gather_kernel(idx_ref, src_ref, o_ref, sem):
    rows = o_ref.shape[0]

    def issue(j, carry):
        pltpu.make_async_copy(src_ref.at[pl.ds(idx_ref[0, 0, j], 1), :], o_ref.at[pl.ds(j, 1), :], sem).start()
        return carry

    lax.fori_loop(0, rows, issue, 0, unroll=8)
    pltpu.make_async_copy(src_ref.at[pl.ds(0, rows), :], o_ref, sem).wait()


def gather_rows(src, idx, rows_per_block=256):
    n = idx.shape[0]
    w = src.shape[1]
    nb = n // rows_per_block
    return pl.pallas_call(
        _gather_kernel,
        grid=(nb,),
        in_specs=[pl.BlockSpec((1, 1, rows_per_block), lambda i: (i, 0, 0), memory_space=pltpu.SMEM),
                  pl.BlockSpec(memory_space=pl.ANY)],
        out_specs=pl.BlockSpec((rows_per_block, w), lambda i: (i, 0)),
        out_shape=jax.ShapeDtypeStruct((n, w), src.dtype),
        scratch_shapes=[pltpu.SemaphoreType.DMA(())],
        compiler_params=_cparams(("arbitrary",)),
        name="gather_rows",
    )(idx.reshape(nb, 1, rows_per_block), src)


def _moe_kernel(be_ref, nu_ref, x_ref, g_ref, wi_ref, bi_ref, wo_ref, bo_ref, o_ref):
    i = pl.program_id(0)
    f = wo_ref.shape[0]

    @pl.when(i < nu_ref[0])
    def _():
        gu = _dot(x_ref[...].astype(BF16), wi_ref[...]) + bi_ref[...]
        gate = jnp.minimum(gu[:, :f], SWIGLU_LIMIT)
        up = jnp.clip(gu[:, f:], -SWIGLU_LIMIT, SWIGLU_LIMIT)
        glu = gate * _sigmoid(SWIGLU_ALPHA * gate)
        act = ((up + 1.0) * glu).astype(BF16)
        y = _dot(act, wo_ref[...]) + bo_ref[...]
        o_ref[...] = y * g_ref[...]

    @pl.when(i >= nu_ref[0])
    def _():
        o_ref[...] = jnp.zeros(o_ref.shape, o_ref.dtype)


def moe_experts(xs, gs, block_e, n_used, w_in, b_in, w_out, b_out):
    p, d = xs.shape
    e, _, f2 = w_in.shape
    f = f2 // 2
    nb = p // MOE_BLOCK
    grid_spec = pltpu.PrefetchScalarGridSpec(
        num_scalar_prefetch=2, grid=(nb,),
        in_specs=[pl.BlockSpec((MOE_BLOCK, d), lambda i, be, nu: (i, 0)),
                  pl.BlockSpec((MOE_BLOCK, 1), lambda i, be, nu: (i, 0)),
                  pl.BlockSpec((None, d, f2), lambda i, be, nu: (be[i], 0, 0)),
                  pl.BlockSpec((None, 1, f2), lambda i, be, nu: (be[i], 0, 0)),
                  pl.BlockSpec((None, f, d), lambda i, be, nu: (be[i], 0, 0)),
                  pl.BlockSpec((None, 1, d), lambda i, be, nu: (be[i], 0, 0))],
        out_specs=pl.BlockSpec((MOE_BLOCK, d), lambda i, be, nu: (i, 0)))
    return pl.pallas_call(
        _moe_kernel,
        grid_spec=grid_spec,
        out_shape=jax.ShapeDtypeStruct((p, d), F32),
        compiler_params=_cparams(("arbitrary",)),
        name="moe_experts",
    )(block_e, n_used, xs, gs, w_in.astype(BF16), b_in.reshape(e, 1, f2), w_out.astype(BF16),
      b_out.reshape(e, 1, d))


def moe_route(route, n_experts):
    t = route.shape[0]
    a = t * TOP_K
    flat_e = route[:, :TOP_K].astype(jnp.int32).reshape(a)
    flat_g = route[:, TOP_K:2 * TOP_K].reshape(a)
    onehot = (flat_e[:, None] == jnp.arange(n_experts, dtype=jnp.int32)[None, :]).astype(jnp.int32)
    csum = jnp.cumsum(onehot, axis=0)
    counts = csum[-1]
    rank = jnp.take_along_axis(csum, flat_e[:, None], axis=1)[:, 0] - 1
    padded = ((counts + MOE_BLOCK - 1) // MOE_BLOCK) * MOE_BLOCK
    pends = jnp.cumsum(padded)
    pstarts = pends - padded
    dest = pstarts[flat_e] + rank
    p = -(-a // MOE_BLOCK) * MOE_BLOCK + n_experts * MOE_BLOCK
    nb = p // MOE_BLOCK
    block_e = jnp.minimum(jnp.searchsorted(pends, jnp.arange(nb, dtype=jnp.int32) * MOE_BLOCK, side="right"),
                          n_experts - 1).astype(jnp.int32)
    n_used = (pends[-1] // MOE_BLOCK).astype(jnp.int32).reshape(1)
    slot_tok = (jnp.arange(p, dtype=jnp.int32) % t).at[dest].set(jnp.arange(a, dtype=jnp.int32) // TOP_K,
                                                                unique_indices=True)
    slot_g = jnp.zeros((p,), F32).at[dest].set(flat_g, unique_indices=True)
    return slot_tok, slot_g.reshape(p, 1), block_e, n_used, dest.astype(jnp.int32)


def moe_ffn(hm, route, w_in, b_in, w_out, b_out):
    t, d = hm.shape
    slot_tok, slot_g, block_e, n_used, dest = moe_route(route, w_in.shape[0])
    xs = gather_rows(hm, slot_tok)
    yb = moe_experts(xs, slot_g, block_e, n_used, w_in, b_in, w_out, b_out)
    yg = gather_rows(yb, dest)
    return yg.reshape(t, TOP_K * d)


def _qkv_kernel(cid_ref, tb_ref, x_ref, yg_ref, g2_ref, sh_ref, sc_ref, g_ref, w_ref, gq_ref, gk_ref,
                cos_ref, sin_ref, ones_ref, xo_ref, q_ref, k_ref, v_ref):
    d = x_ref.shape[1]
    yg = yg_ref[...]
    moe = yg[:, 0:d] + yg[:, d:2 * d] + yg[:, 2 * d:3 * d] + yg[:, 3 * d:4 * d]
    x2 = x_ref[...] + g2_ref[...] * moe
    xo_ref[...] = x2
    h = _modulate(x2, g_ref[...], sh_ref[...], sc_ref[...]).astype(BF16)
    lane = lax.broadcasted_iota(jnp.int32, (TM, LANES), 1)
    first_half = (lane % HEAD_DIM) < (HEAD_DIM // 2)
    cos = cos_ref[...]
    sin = sin_ref[...]

    def norm_rope(raw, gain):
        ss = _dot((raw * raw).astype(BF16), ones_ref[...]) * (1.0 / HEAD_DIM)
        y = raw * lax.rsqrt(ss + EPS) * gain
        rot = jnp.where(first_half, pltpu.roll(y, LANES - HEAD_DIM // 2, 1), pltpu.roll(y, HEAD_DIM // 2, 1))
        return y * cos + rot * sin

    for c in range(d // LANES):
        lo = c * LANES
        q = norm_rope(_dot(h, w_ref[:, lo:lo + LANES]), gq_ref[...])
        q_ref[:, lo:lo + LANES] = (q * (HEAD_DIM ** -0.5)).astype(BF16)
        k = norm_rope(_dot(h, w_ref[:, d + lo:d + lo + LANES]), gk_ref[...])
        k_ref[:, lo:lo + LANES] = k.astype(BF16)
        v_ref[:, lo:lo + LANES] = _dot(h, w_ref[:, 2 * d + lo:2 * d + lo + LANES]).astype(BF16)


def attn_qkv(x, yg, mod, layer, meta, g, w_qkv, g_q, g_k, cos_t, sin_t):
    t, d = x.shape
    blk = pl.BlockSpec((TM, d), lambda i, *_: (i, 0))
    ones = jnp.asarray(np.kron(np.eye(LANES // HEAD_DIM), np.ones((HEAD_DIM, HEAD_DIM))), BF16)
    rep = LANES // HEAD_DIM
    tspec = pl.BlockSpec((TM, LANES), lambda i, cid, tb: (tb[i], 0))
    grid_spec = pltpu.PrefetchScalarGridSpec(
        num_scalar_prefetch=2, grid=(t // TM,),
        in_specs=[blk, pl.BlockSpec((TM, TOP_K * d), lambda i, *_: (i, 0)),
                  _mod_spec(layer - 1, 5, d), _mod_spec(layer, 0, d), _mod_spec(layer, 1, d), _row_spec(d),
                  pl.BlockSpec((d, 3 * d), lambda i, *_: (0, 0)), _row_spec(LANES), _row_spec(LANES),
                  tspec, tspec, pl.BlockSpec((LANES, LANES), lambda i, *_: (0, 0))],
        out_specs=[blk, blk, blk, blk])
    bf = jax.ShapeDtypeStruct((t, d), BF16)
    return pl.pallas_call(
        _qkv_kernel,
        grid_spec=grid_spec,
        out_shape=[jax.ShapeDtypeStruct((t, d), F32), bf, bf, bf],
        compiler_params=_cparams(("parallel",)),
        name="attn_qkv",
    )(meta["cid"], meta["rope_blk"], x, yg, mod, mod, mod, g.reshape(1, d), w_qkv.astype(BF16),
      jnp.tile(g_q, rep).reshape(1, LANES), jnp.tile(g_k, rep).reshape(1, LANES), cos_t, sin_t, ones)


def rope_tables(n_tokens):
    pairs = HEAD_DIM // 4
    n_rows = n_tokens // GRID_W
    row = jnp.repeat(jnp.arange(n_rows, dtype=F32), GRID_W)
    col = jnp.tile(jnp.arange(GRID_W, dtype=F32), n_rows)
    inv = ROPE_THETA ** (-jnp.arange(pairs, dtype=F32) / pairs)
    ang = jnp.concatenate([row[:, None] * inv, col[:, None] * inv], axis=-1)
    cos, sin = jnp.cos(ang), jnp.sin(ang)
    reps = LANES // HEAD_DIM
    cos_t = jnp.tile(jnp.concatenate([cos, cos], axis=-1), (1, reps))
    sin_t = jnp.tile(jnp.concatenate([-sin, sin], axis=-1), (1, reps))
    cos_t = jnp.concatenate([cos_t, jnp.ones((TM, LANES), F32)], axis=0)
    sin_t = jnp.concatenate([sin_t, jnp.zeros((TM, LANES), F32)], axis=0)
    return cos_t, sin_t


def _attn_kernel(lv_ref, q_ref, kt_ref, v_ref, gs_ref, o_ref, *, lam_init):
    lv = lv_ref[...]
    lam = (jnp.exp(jnp.sum(lv[0:1] * lv[1:2], axis=-1, keepdims=True))
           - jnp.exp(jnp.sum(lv[2:3] * lv[3:4], axis=-1, keepdims=True)) + lam_init)
    q = q_ref[...]
    lane = lax.broadcasted_iota(jnp.int32, q.shape, 1)
    zero = jnp.zeros(q.shape, q.dtype)
    kt = kt_ref[...]
    s1 = _dot(jnp.where(lane < HEAD_DIM, q, zero), kt)
    s2 = _dot(jnp.where(lane >= HEAD_DIM, q, zero), kt)
    e1 = jnp.exp(s1 - jnp.max(s1, axis=-1, keepdims=True))
    e2 = jnp.exp(s2 - jnp.max(s2, axis=-1, keepdims=True))
    r1 = 1.0 / jnp.sum(e1, axis=-1, keepdims=True)
    r2 = lam / jnp.sum(e2, axis=-1, keepdims=True)
    a = (e1 * r1 - e2 * r2).astype(BF16)
    o = _dot(a, v_ref[...])
    ms = jnp.mean(o * o, axis=-1, keepdims=True)
    o_ref[...] = (o * lax.rsqrt(ms + EPS) * gs_ref[...] * (1.0 - lam_init)).astype(BF16)


def diff_attention(q, kt, v, lvec, g_sub, lam_init, nbatch, lq, q_row0, v_batched):
    t, d = q.shape
    lk = kt.shape[2]
    nh = d // V_DIM
    tq = 256
    nq = lq // tq
    qb0 = q_row0 // tq
    if v_batched:
        vspec = pl.BlockSpec((None, lk, V_DIM), lambda b, h, i: (b, 0, h))
    else:
        vspec = pl.BlockSpec((lk, V_DIM), lambda b, h, i: (q_row0 // lk + b, h))
    in_specs = [pl.BlockSpec((8, LANES), lambda b, h, i: (0, 0)),
                pl.BlockSpec((tq, V_DIM), lambda b, h, i: (qb0 + b * nq + i, h)),
                pl.BlockSpec((None, V_DIM, lk), lambda b, h, i: (b, h, 0)),
                vspec,
                pl.BlockSpec((1, V_DIM), lambda b, h, i: (0, 0))]
    return pl.pallas_call(
        functools.partial(_attn_kernel, lam_init=lam_init),
        grid=(nbatch, nh, nq),
        in_specs=in_specs,
        out_specs=pl.BlockSpec((tq, V_DIM), lambda b, h, i: (b * nq + i, h)),
        out_shape=jax.ShapeDtypeStruct((nbatch * lq, d), BF16),
        compiler_params=_cparams(("parallel", "parallel", "arbitrary")),
        name="diff_attention",
    )(lvec, q, kt, v, g_sub.reshape(1, V_DIM))


def _final_kernel(cid_ref, x_ref, yg_ref, g2_ref, o_ref):
    d = x_ref.shape[1]
    yg = yg_ref[...]
    moe = yg[:, 0:d] + yg[:, d:2 * d] + yg[:, 2 * d:3 * d] + yg[:, 3 * d:4 * d]
    o_ref[...] = x_ref[...] + g2_ref[...] * moe


def final_residual(x, yg, mod, layer, meta):
    t, d = x.shape
    blk = pl.BlockSpec((TM, d), lambda i, *_: (i, 0))
    grid_spec = pltpu.PrefetchScalarGridSpec(
        num_scalar_prefetch=1, grid=(t // TM,),
        in_specs=[blk, pl.BlockSpec((TM, TOP_K * d), lambda i, *_: (i, 0)), _mod_spec(layer, 5, d)],
        out_specs=blk)
    return pl.pallas_call(
        _final_kernel, grid_spec=grid_spec, out_shape=jax.ShapeDtypeStruct((t, d), F32),
        compiler_params=_cparams(("parallel",)), name="final_residual",
    )(meta["cid"], x, yg, mod)


def _block_meta(n_ctx, seq, n_lat, dec_seq):
    cb, lb = seq // TM, dec_seq // TM
    cid = np.concatenate([np.zeros(n_ctx * cb, np.int32), 1 + np.repeat(np.arange(n_lat, dtype=np.int32), lb)])
    pos = np.concatenate([np.tile(np.arange(cb), n_ctx), np.tile(np.arange(lb), n_lat)])
    last = np.concatenate([np.full(n_ctx * cb, cb - 1), np.full(n_lat * lb, lb - 1)])
    rope_blk = np.concatenate([np.full(n_ctx * cb, lb), np.tile(np.arange(lb), n_lat)])
    return {"cid": jnp.asarray(cid), "has_prev": jnp.asarray((pos > 0).astype(np.int32)),
            "has_next": jnp.asarray((pos < last).astype(np.int32)),
            "rope_blk": jnp.asarray(rope_blk.astype(np.int32))}


def kernel(x_prompt, x_sample, cache_k, cache_v, c, c_ctx, w_mod, b_mod, g_mix, g_ffn, hy_w_in, hy_b_in, hy_w_short, hy_b_short, hy_f_w1, hy_f_b1, hy_f_w2, hy_f_b2, hy_f_w3, hy_f_freq, hy_d_skip, hy_w_out, hy_b_out, at_w_qkv, at_g_q, at_g_k, at_lq1, at_lk1, at_lq2, at_lk2, at_g_sub, at_w_o, moe_w_router, moe_b_router, moe_w_in, moe_b_in, moe_w_out, moe_b_out):
    n_ctx, seq, d = x_prompt.shape
    n_lat, dec_seq, _ = x_sample.shape
    past = cache_k.shape[2]
    depth = w_mod.shape[0]
    n_heads = d // V_DIM
    t_ctx, t_lat = n_ctx * seq, n_lat * dec_seq
    assert seq % TM == 0 and dec_seq % TM == 0 and dec_seq % seq == 0 and t_ctx % dec_seq == 0
    assert 1 + n_lat <= N_COND and depth % 2 == 0
    meta = _block_meta(n_ctx, seq, n_lat, dec_seq)

    x = jnp.concatenate([x_prompt.reshape(t_ctx, d), x_sample.reshape(t_lat, d)], axis=0)
    cond = jnp.zeros((N_COND, d), F32).at[0].set(c_ctx).at[1:1 + n_lat].set(c)
    mod = adaln_params(cond, w_mod, b_mod)
    cos_t, sin_t = rope_tables(dec_seq)

    yg = None
    new_k, new_v = [], []
    for i in range(depth):
        j = i // 2
        if i % 2 == 0:
            if yg is not None:
                x = final_residual(x, yg, mod, i - 1, meta)
            x0, zin = hyena_in(x, mod, i, meta, g_mix[i], hy_w_in[j], hy_b_in[j], hy_w_short[j], hy_b_short[j])
            ys = []
            for (l, nb, row0) in ((seq, n_ctx, 0), (dec_seq, n_lat, t_ctx)):
                b = min(CONV_CHUNK, l)
                fwd_np, seg_np = _dft_constants(b)
                taps = hyena_taps(l, hy_f_w1[j], hy_f_b1[j], hy_f_w2[j], hy_f_b2[j], hy_f_w3[j], hy_f_freq[j])
                spectra = hyena_spectra(taps, b, jnp.asarray(seg_np).astype(BF16))
                ys.append(hyena_long_conv(zin, spectra, jnp.asarray(fwd_np).astype(BF16), l, nb, row0))
            x, hm, route = mixer_tail(x, ys[0], ys[1], (zin, x0), mod, i, meta, hy_w_out[j], hy_b_out[j], g_ffn[i],
                                      moe_w_router[i], moe_b_router[i], d_skip=hy_d_skip[j])
        else:
            lam_init = 0.8 - 0.6 * math.exp(-0.3 * i)
            x, q, k, v = attn_qkv(x, yg, mod, i, meta, g_mix[i], at_w_qkv[j], at_g_q[j], at_g_k[j], cos_t, sin_t)
            new_k.append(k[:t_ctx].astype(F32).reshape(n_ctx, seq, n_heads, V_DIM))
            new_v.append(v[:t_ctx].astype(F32).reshape(n_ctx, seq, n_heads, V_DIM))
            lvec = jnp.zeros((8, LANES), F32).at[0:4, :HEAD_DIM].set(
                jnp.stack([at_lq1[j], at_lk1[j], at_lq2[j], at_lk2[j]]))
            kt_ctx = k[:t_ctx].reshape(n_ctx, seq, d).transpose(0, 2, 1)
            k_lat = jnp.concatenate([k[t_ctx:].reshape(n_lat, dec_seq, d),
                                     cache_k[:, j].reshape(n_lat, past, d).astype(BF16)], axis=1)
            kt_lat = k_lat.transpose(0, 2, 1)
            v_lat = jnp.concatenate([v[t_ctx:].reshape(n_lat, dec_seq, d),
                                     cache_v[:, j].reshape(n_lat, past, d).astype(BF16)], axis=1)
            o_ctx = diff_attention(q, kt_ctx, v, lvec, at_g_sub[j], lam_init, n_ctx, seq, 0, False)
            o_lat = diff_attention(q, kt_lat, v_lat, lvec, at_g_sub[j], lam_init, n_lat, dec_seq, t_ctx, True)
            x, hm, route = mixer_tail(x, o_ctx, o_lat, (), mod, i, meta, at_w_o[j], None, g_ffn[i],
                                      moe_w_router[i], moe_b_router[i])
        yg = moe_ffn(hm, route, moe_w_in[i], moe_b_in[i], moe_w_out[i], moe_b_out[i])
    x = final_residual(x, yg, mod, depth - 1, meta)
    y_prompt = x[:t_ctx].reshape(n_ctx, seq, d)
    y_sample = x[t_ctx:].reshape(n_lat, dec_seq, d)
    return (y_prompt, y_sample, jnp.stack(new_k, axis=1), jnp.stack(new_v, axis=1))
```
